```python
import jax, jax.numpy as jnp
from jax import lax
import numpy as np

D_MODEL = 1024
BATCH = 16
SEQ = 4096
DEPTH = 4

N_A_LAYERS = DEPTH // 2
N_B_LAYERS = DEPTH - N_A_LAYERS

CHUNK = 128
A_WIDTH = 2 * D_MODEL
A_GROUPS = 8
A_GROUP_CH = A_WIDTH // A_GROUPS

HEAD_DIM = 64
N_Q_HEADS = D_MODEL // HEAD_DIM
N_KV_HEADS = 4
Q_PER_KV = N_Q_HEADS // N_KV_HEADS
B_WIDTH = N_Q_HEADS * HEAD_DIM
KV_WIDTH = N_KV_HEADS * HEAD_DIM
WINDOW = 128
BLOCK = 128

EPS = 1e-6

kernel_name = "yoco_gmlp_swa_sink_hybrid"


def rms_norm(x, g):
    xf = x.astype(jnp.float32)
    y = xf * lax.rsqrt(jnp.mean(xf * xf, axis=-1, keepdims=True) + EPS)
    return (y * g.astype(jnp.float32)).astype(x.dtype)


def layer_norm(x, g, b):
    xf = x.astype(jnp.float32)
    mu = jnp.mean(xf, axis=-1, keepdims=True)
    var = jnp.mean(jnp.square(xf - mu), axis=-1, keepdims=True)
    y = (xf - mu) * lax.rsqrt(var + EPS)
    return (y * g.astype(jnp.float32) + b.astype(jnp.float32)).astype(x.dtype)


def alibi_slopes(n_heads):
    return jnp.asarray(np.array([2.0 ** (-8.0 * (h + 1) / n_heads) for h in range(n_heads)], dtype=np.float32))


def gmlp_mixer(h, w_in, ln_g, ln_b, w_s, b_s, w_out):
    bsz, s_len, _ = h.shape
    u, v, z = jnp.split(h @ w_in, 3, axis=-1)
    u = jax.nn.gelu(u)
    v = layer_norm(jax.nn.gelu(v), ln_g, ln_b)
    n_chunks = s_len // CHUNK
    v = v.reshape(bsz, n_chunks, CHUNK, A_GROUPS, A_GROUP_CH)
    causal = jnp.tril(jnp.ones((CHUNK, CHUNK), dtype=bool))
    w_causal = jnp.where(causal[None], w_s, 0.0).astype(v.dtype)
    s = jnp.einsum('gts,bnsgc->bntgc', w_causal, v) + b_s.T[:, :, None].astype(v.dtype)
    s = s.reshape(bsz, s_len, A_WIDTH)
    y = u * s * jax.nn.silu(z)
    return y @ w_out


def swa_mixer(h, k, v, w_in, sinks, w_out):
    bsz, s_len, _ = h.shape
    q, z = jnp.split(h @ w_in, 2, axis=-1)
    q = q * (HEAD_DIM ** -0.5)
    n_blocks = s_len // BLOCK
    q_blocks = q.reshape(bsz, n_blocks, BLOCK, N_KV_HEADS, Q_PER_KV, HEAD_DIM).transpose(1, 0, 2, 3, 4, 5)
    pad = jnp.zeros((bsz, BLOCK, N_KV_HEADS, HEAD_DIM), k.dtype)
    k_pad = jnp.concatenate([pad, k], axis=1)
    v_pad = jnp.concatenate([pad, v], axis=1)

    r = jnp.arange(BLOCK)[:, None]
    c = jnp.arange(2 * BLOCK)[None, :]
    dist = BLOCK + r - c
    in_window = (dist >= 0) & (dist < WINDOW)
    slopes = alibi_slopes(N_Q_HEADS).reshape(N_KV_HEADS, Q_PER_KV)
    alibi = -slopes[:, :, None, None] * dist.astype(jnp.float32)
    sink = sinks.astype(jnp.float32).reshape(N_KV_HEADS, Q_PER_KV)[None, :, :, None]

    def block_fn(args):
        i, qb = args
        start = i * BLOCK
        kb = lax.dynamic_slice_in_dim(k_pad, start, 2 * BLOCK, axis=1)
        vb = lax.dynamic_slice_in_dim(v_pad, start, 2 * BLOCK, axis=1)
        logits = jnp.einsum('btkgd,bskd->bkgts', qb, kb).astype(jnp.float32) + alibi
        valid = in_window & ((start - BLOCK + c) >= 0)
        logits = jnp.where(valid, logits, -jnp.inf)
        m = jnp.maximum(jnp.max(logits, axis=-1), sink)
        p = jnp.exp(logits - m[..., None])
        denom = jnp.sum(p, axis=-1) + jnp.exp(sink - m)
        o = jnp.einsum('bkgts,bskd->btkgd', p.astype(vb.dtype), vb).astype(jnp.float32)
        o = o / denom.transpose(0, 3, 1, 2)[..., None]
        return o.astype(h.dtype)

    out = lax.map(block_fn, (jnp.arange(n_blocks), q_blocks))
    out = out.transpose(1, 0, 2, 3, 4, 5).reshape(bsz, s_len, B_WIDTH)
    y = out * jax.nn.silu(z)
    return y @ w_out


def setup_inputs(seed: int = 0) -> dict:
    key = jax.random.key(seed)
    ks = jax.random.split(key, 20)
    f32 = jnp.float32
    nrm = lambda k, shape, scale: jax.random.normal(k, shape, f32) * scale
    x = jax.random.normal(ks[0], (BATCH, SEQ, D_MODEL), f32)
    a_w_in = nrm(ks[1], (N_A_LAYERS, D_MODEL, 3 * A_WIDTH), D_MODEL ** -0.5)
    a_ln_g = 1.0 + nrm(ks[2], (N_A_LAYERS, A_WIDTH), 0.05)
    a_ln_b = nrm(ks[3], (N_A_LAYERS, A_WIDTH), 0.02)
    a_w_s = nrm(ks[4], (N_A_LAYERS, A_GROUPS, CHUNK, CHUNK), CHUNK ** -0.5)
    a_b_s = 1.0 + nrm(ks[5], (N_A_LAYERS, A_GROUPS, CHUNK), 0.1)
    a_w_out = nrm(ks[6], (N_A_LAYERS, A_WIDTH, D_MODEL), A_WIDTH ** -0.5)
    a_pre_g = 1.0 + nrm(ks[7], (N_A_LAYERS, D_MODEL), 0.05)
    a_post_g = 1.0 + nrm(ks[8], (N_A_LAYERS, D_MODEL), 0.05)
    kv_norm_g = 1.0 + nrm(ks[9], (D_MODEL,), 0.05)
    w_k = nrm(ks[10], (D_MODEL, KV_WIDTH), D_MODEL ** -0.5)
    w_v = nrm(ks[11], (D_MODEL, KV_WIDTH), D_MODEL ** -0.5)
    b_w_in = nrm(ks[12], (N_B_LAYERS, D_MODEL, 2 * B_WIDTH), D_MODEL ** -0.5)
    b_sinks = nrm(ks[13], (N_B_LAYERS, N_Q_HEADS), 0.5)
    b_w_out = nrm(ks[14], (N_B_LAYERS, B_WIDTH, D_MODEL), B_WIDTH ** -0.5)
    b_pre_g = 1.0 + nrm(ks[15], (N_B_LAYERS, D_MODEL), 0.05)
    b_post_g = 1.0 + nrm(ks[16], (N_B_LAYERS, D_MODEL), 0.05)
    return {"x": x, "a_w_in": a_w_in, "a_ln_g": a_ln_g, "a_ln_b": a_ln_b, "a_w_s": a_w_s,
            "a_b_s": a_b_s, "a_w_out": a_w_out, "a_pre_g": a_pre_g, "a_post_g": a_post_g,
            "kv_norm_g": kv_norm_g, "w_k": w_k, "w_v": w_v, "b_w_in": b_w_in, "b_sinks": b_sinks,
            "b_w_out": b_w_out, "b_pre_g": b_pre_g, "b_post_g": b_post_g}


def reference(x, a_w_in, a_ln_g, a_ln_b, a_w_s, a_b_s, a_w_out, a_pre_g, a_post_g,
              kv_norm_g, w_k, w_v, b_w_in, b_sinks, b_w_out, b_pre_g, b_post_g):
    bsz, s_len, _ = x.shape
    h = x
    k_shared = None
    v_shared = None
    for layer in range(DEPTH):
        if layer < N_A_LAYERS:
            i = layer
            y = gmlp_mixer(rms_norm(h, a_pre_g[i]), a_w_in[i], a_ln_g[i], a_ln_b[i],
                           a_w_s[i], a_b_s[i], a_w_out[i])
            h = h + rms_norm(y, a_post_g[i])
            if layer == N_A_LAYERS - 1:
                kv_in = rms_norm(h, kv_norm_g)
                k_shared = (kv_in @ w_k).reshape(bsz, s_len, N_KV_HEADS, HEAD_DIM)
                v_shared = (kv_in @ w_v).reshape(bsz, s_len, N_KV_HEADS, HEAD_DIM)
        else:
            j = layer - N_A_LAYERS
            y = swa_mixer(rms_norm(h, b_pre_g[j]), k_shared, v_shared,
                          b_w_in[j], b_sinks[j], b_w_out[j])
            h = h + rms_norm(y, b_post_g[j])
    return h
```

```python
import functools

import numpy as np
import jax
import jax.numpy as jnp
from jax import lax
from jax.experimental import pallas as pl
from jax.experimental.pallas import tpu as pltpu

D_MODEL = 1024
CHUNK = 128
A_WIDTH = 2 * D_MODEL
A_GROUPS = 8
A_GROUP_CH = A_WIDTH // A_GROUPS
HEAD_DIM = 64
N_Q_HEADS = 16
N_KV_HEADS = 4
Q_PER_KV = N_Q_HEADS // N_KV_HEADS
KV_WIDTH = N_KV_HEADS * HEAD_DIM
EPS = 1e-6

TOKEN_TILE = 512
V7X_VMEM_LIMIT_BYTES = 56 * 1024 * 1024

_BF16 = jnp.bfloat16
_F32 = jnp.float32


def _rms_scale(x):
    return lax.rsqrt(jnp.mean(x * x, axis=-1, keepdims=True) + EPS)


def _dot(a, b):
    return jnp.dot(a, b, preferred_element_type=_F32)


def _resident(shape):
    zeros = (0,) * len(shape)
    return pl.BlockSpec(shape, lambda i: zeros, pipeline_mode=pl.Buffered(1))


def _gmlp_layer_kernel(h_ref, pre_g_ref, wu_ref, wv_ref, wz_ref, ln_g_ref, ln_b_ref, ws_ref,
                       bs_ref, wout_ref, post_g_ref, *rest, with_kv):
    if with_kv:
        kv_g_ref, wkt_ref, wvp_ref, out_ref, kt_ref, vp_ref, v_scr = rest
    else:
        out_ref, v_scr = rest
    tm = h_ref.shape[0]
    n_chunks = tm // CHUNK

    h = h_ref[...]
    xn = (h * _rms_scale(h) * pre_g_ref[...]).astype(_BF16)

    row_sum = jnp.zeros((tm, 1), _F32)
    for g in range(A_GROUPS):
        vg = jax.nn.gelu(_dot(xn, wv_ref[g]))
        v_scr[g] = vg
        row_sum = row_sum + jnp.sum(vg, axis=-1, keepdims=True)
    mu = row_sum * (1.0 / A_WIDTH)
    sq_sum = jnp.zeros((tm, 1), _F32)
    for g in range(A_GROUPS):
        d = v_scr[g] - mu
        sq_sum = sq_sum + jnp.sum(d * d, axis=-1, keepdims=True)
    rstd = lax.rsqrt(sq_sum * (1.0 / A_WIDTH) + EPS)

    row = lax.broadcasted_iota(jnp.int32, (CHUNK, CHUNK), 0)
    col = lax.broadcasted_iota(jnp.int32, (CHUNK, CHUNK), 1)
    causal = col <= row

    acc = jnp.zeros((tm, D_MODEL), _F32)
    for g in range(A_GROUPS):
        vn = ((v_scr[g] - mu) * rstd * ln_g_ref[g] + ln_b_ref[g]).astype(_BF16)
        w_causal = jnp.where(causal, ws_ref[g], 0.0).astype(_BF16)
        bias = jnp.concatenate([bs_ref[g]] * (A_GROUP_CH // CHUNK), axis=1)
        s = jnp.concatenate(
            [_dot(w_causal, vn[c * CHUNK:(c + 1) * CHUNK]) + bias for c in range(n_chunks)], axis=0)
        u = jax.nn.gelu(_dot(xn, wu_ref[g]))
        z = _dot(xn, wz_ref[g])
        y = (u * s * jax.nn.silu(z)).astype(_BF16)
        acc = acc + _dot(y, wout_ref[g])

    h_new = h + acc * _rms_scale(acc) * post_g_ref[...]
    out_ref[...] = h_new

    if with_kv:
        kvn = (h_new * _rms_scale(h_new) * kv_g_ref[...]).astype(_BF16)
        kt = lax.dot_general(wkt_ref[...], kvn, (((1,), (1,)), ((), ())), preferred_element_type=_F32)
        kt_ref[...] = kt.astype(_BF16)
        vp_ref[...] = _dot(kvn, wvp_ref[...]).astype(_BF16)


def _gmlp_layer(h, pre_g, w_in, ln_g, ln_b, w_s, b_s, w_out, post_g, kv=None):
    n_tok = h.shape[0]
    tm = TOKEN_TILE
    grouped_cols = lambda w: w.reshape(D_MODEL, A_GROUPS, A_GROUP_CH).transpose(1, 0, 2).astype(_BF16)
    wu = grouped_cols(w_in[:, :A_WIDTH])
    wv = grouped_cols(w_in[:, A_WIDTH:2 * A_WIDTH])
    wz = grouped_cols(w_in[:, 2 * A_WIDTH:])
    wout = w_out.reshape(A_GROUPS, A_GROUP_CH, D_MODEL).astype(_BF16)
    bs = jnp.broadcast_to(b_s[:, :, None], (A_GROUPS, CHUNK, CHUNK)).astype(_F32)

    args = [h, pre_g.reshape(1, D_MODEL), wu, wv, wz, ln_g.reshape(A_GROUPS, 1, A_GROUP_CH),
            ln_b.reshape(A_GROUPS, 1, A_GROUP_CH), w_s, bs, wout, post_g.reshape(1, D_MODEL)]
    tile = pl.BlockSpec((tm, D_MODEL), lambda i: (i, 0))
    in_specs = [tile] + [_resident(a.shape) for a in args[1:]]
    out_shape = [jax.ShapeDtypeStruct((n_tok, D_MODEL), _F32)]
    out_specs = [tile]
    if kv is not None:
        kv_g, w_k, w_v = kv
        extra = [kv_g.reshape(1, D_MODEL), w_k.T.astype(_BF16), w_v.astype(_BF16)]
        args += extra
        in_specs += [_resident(a.shape) for a in extra]
        out_shape += [jax.ShapeDtypeStruct((KV_WIDTH, n_tok), _BF16),
                      jax.ShapeDtypeStruct((n_tok, KV_WIDTH), _BF16)]
        out_specs += [pl.BlockSpec((KV_WIDTH, tm), lambda i: (0, i)),
                      pl.BlockSpec((tm, KV_WIDTH), lambda i: (i, 0))]

    outs = pl.pallas_call(
        functools.partial(_gmlp_layer_kernel, with_kv=kv is not None),
        grid=(n_tok // tm,),
        in_specs=in_specs,
        out_specs=out_specs,
        out_shape=out_shape,
        scratch_shapes=[pltpu.VMEM((A_GROUPS, tm, A_GROUP_CH), _F32)],
        compiler_params=pltpu.CompilerParams(
            dimension_semantics=("arbitrary",), vmem_limit_bytes=V7X_VMEM_LIMIT_BYTES),
        name="gmlp_kv_layer" if kv is not None else "gmlp_layer",
    )(*args)
    return outs if kv is not None else outs[0]


def _swa_layer_kernel(sinks_ref, h_ref, pre_g_ref, wq_ref, wz_ref, ktp_ref, ktc_ref, vpp_ref,
                      vpc_ref, alibi_ref, wout_ref, post_g_ref, out_ref, *, blocks_per_seq):
    tm = h_ref.shape[0]
    n_blocks = tm // CHUNK

    h = h_ref[...]
    xn = (h * _rms_scale(h) * pre_g_ref[...]).astype(_BF16)
    q = (_dot(xn, wq_ref[...]) * (HEAD_DIM ** -0.5)).astype(_BF16)
    z = _dot(xn, wz_ref[...])

    kt_all = jnp.concatenate([ktp_ref[...], ktc_ref[...]], axis=1)
    v_all = jnp.concatenate([vpp_ref[...], vpc_ref[...]], axis=0)
    kt_band = lax.broadcasted_iota(jnp.int32, kt_all.shape, 0) // HEAD_DIM
    v_band = lax.broadcasted_iota(jnp.int32, v_all.shape, 1) // HEAD_DIM
    kt_heads = [jnp.where(kt_band == kv, kt_all, jnp.zeros_like(kt_all)) for kv in range(N_KV_HEADS)]
    v_heads = [jnp.where(v_band == kv, v_all, jnp.zeros_like(v_all)) for kv in range(N_KV_HEADS)]

    r = lax.broadcasted_iota(jnp.int32, (CHUNK, 2 * CHUNK), 0)
    c = lax.broadcasted_iota(jnp.int32, (CHUNK, 2 * CHUNK), 1)
    dist = CHUNK + r - c
    in_window = (dist >= 0) & (dist < CHUNK)
    first_in_seq = (pl.program_id(0) * n_blocks) % blocks_per_seq == 0
    in_window_first = in_window & ((c >= CHUNK) | jnp.logical_not(first_in_seq))

    o_blocks = []
    for b in range(n_blocks):
        valid = in_window_first if b == 0 else in_window
        keys = slice(b * CHUNK, (b + 2) * CHUNK)
        o_groups = []
        for g in range(Q_PER_KV):
            qg = q[b * CHUNK:(b + 1) * CHUNK, g * KV_WIDTH:(g + 1) * KV_WIDTH]
            og = jnp.zeros((CHUNK, KV_WIDTH), _F32)
            for kv in range(N_KV_HEADS):
                head = kv * Q_PER_KV + g
                sink = sinks_ref[head]
                logits = _dot(qg, kt_heads[kv][:, keys]) + alibi_ref[head]
                logits = jnp.where(valid, logits, -jnp.inf)
                m = jnp.maximum(jnp.max(logits, axis=-1, keepdims=True), sink)
                p = jnp.exp(logits - m)
                denom = jnp.sum(p, axis=-1, keepdims=True) + jnp.exp(sink - m)
                pn = (p * (1.0 / denom)).astype(_BF16)
                og = og + _dot(pn, v_heads[kv][keys, :])
            o_groups.append(og)
        o_blocks.append(jnp.concatenate(o_groups, axis=1))
    o = jnp.concatenate(o_blocks, axis=0)

    y = (o * jax.nn.silu(z)).astype(_BF16)
    out = _dot(y, wout_ref[...])
    out_ref[...] = h + out * _rms_scale(out) * post_g_ref[...]


def _head_permutation():
    g, kv, d = np.meshgrid(np.arange(Q_PER_KV), np.arange(N_KV_HEADS), np.arange(HEAD_DIM), indexing="ij")
    return ((kv * Q_PER_KV + g) * HEAD_DIM + d).reshape(-1)


def _alibi_table():
    r = np.arange(CHUNK)[:, None]
    c = np.arange(2 * CHUNK)[None, :]
    dist = (CHUNK + r - c).astype(np.float32)
    slopes = np.array([2.0 ** (-8.0 * (hd + 1) / N_Q_HEADS) for hd in range(N_Q_HEADS)], dtype=np.float32)
    return -slopes[:, None, None] * dist[None]


def _swa_layer(h, kt, vp, pre_g, w_in, sinks, w_out, post_g, seq_len):
    n_tok = h.shape[0]
    tm = TOKEN_TILE
    blocks_per_tile = tm // CHUNK
    perm = _head_permutation()
    wq = w_in[:, :D_MODEL][:, perm].astype(_BF16)
    wz = w_in[:, D_MODEL:][:, perm].astype(_BF16)
    wout = w_out[perm, :].astype(_BF16)
    alibi = jnp.asarray(_alibi_table())

    tile = pl.BlockSpec((tm, D_MODEL), lambda i, s: (i, 0))
    const = lambda shape: pl.BlockSpec(shape, lambda i, s: (0,) * len(shape), pipeline_mode=pl.Buffered(1))
    prev_block = lambda i: jnp.maximum(i * blocks_per_tile - 1, 0)
    in_specs = [
        tile,
        const((1, D_MODEL)),
        const(wq.shape),
        const(wz.shape),
        pl.BlockSpec((KV_WIDTH, CHUNK), lambda i, s: (0, prev_block(i))),
        pl.BlockSpec((KV_WIDTH, tm), lambda i, s: (0, i)),
        pl.BlockSpec((CHUNK, KV_WIDTH), lambda i, s: (prev_block(i), 0)),
        pl.BlockSpec((tm, KV_WIDTH), lambda i, s: (i, 0)),
        const(alibi.shape),
        const(wout.shape),
        const((1, D_MODEL)),
    ]
    return pl.pallas_call(
        functools.partial(_swa_layer_kernel, blocks_per_seq=seq_len // CHUNK),
        grid_spec=pltpu.PrefetchScalarGridSpec(
            num_scalar_prefetch=1,
            grid=(n_tok // tm,),
            in_specs=in_specs,
            out_specs=tile,
        ),
        out_shape=jax.ShapeDtypeStruct((n_tok, D_MODEL), _F32),
        compiler_params=pltpu.CompilerParams(
            dimension_semantics=("arbitrary",), vmem_limit_bytes=V7X_VMEM_LIMIT_BYTES),
        name="swa_layer",
    )(sinks.astype(_F32), h, pre_g.reshape(1, D_MODEL), wq, wz, kt, kt, vp, vp, alibi, wout,
      post_g.reshape(1, D_MODEL))


def kernel(x, a_w_in, a_ln_g, a_ln_b, a_w_s, a_b_s, a_w_out, a_pre_g, a_post_g, kv_norm_g, w_k, w_v,
           b_w_in, b_sinks, b_w_out, b_pre_g, b_post_g):
    bsz, seq_len, d_model = x.shape
    assert d_model == D_MODEL and seq_len % TOKEN_TILE == 0 and TOKEN_TILE % CHUNK == 0
    n_a = a_w_in.shape[0]
    h = x.reshape(bsz * seq_len, d_model)
    kt = vp = None
    for i in range(n_a):
        kv = (kv_norm_g, w_k, w_v) if i == n_a - 1 else None
        res = _gmlp_layer(h, a_pre_g[i], a_w_in[i], a_ln_g[i], a_ln_b[i], a_w_s[i], a_b_s[i],
                          a_w_out[i], a_post_g[i], kv=kv)
        if kv is None:
            h = res
        else:
            h, kt, vp = res
    for j in range(b_w_in.shape[0]):
        h = _swa_layer(h, kt, vp, b_pre_g[j], b_w_in[j], b_sinks[j], b_w_out[j], b_post_g[j], seq_len)
    return h.reshape(bsz, seq_len, d_model)
```

```python
import functools

import numpy as np
import jax
import jax.numpy as jnp
from jax import lax
from jax.experimental import pallas as pl
from jax.experimental.pallas import tpu as pltpu

D_MODEL = 1024
CHUNK = 128
A_WIDTH = 2 * D_MODEL
A_GROUPS = 8
A_GROUP_CH = A_WIDTH // A_GROUPS
A_GROUPS_PER_DOT = 2
V_DOT_WIDTH = 1024
HEAD_DIM = 64
N_Q_HEADS = 16
N_KV_HEADS = 4
Q_PER_KV = N_Q_HEADS // N_KV_HEADS
KV_WIDTH = N_KV_HEADS * HEAD_DIM
EPS = 1e-6

TOKEN_TILE = 512
V7X_VMEM_LIMIT_BYTES = 56 * 1024 * 1024

_BF16 = jnp.bfloat16
_F32 = jnp.float32

_GELU_K1 = 0.7978845608028654
_GELU_K2 = _GELU_K1 * 0.044715


def _rms_scale(x):
    return lax.rsqrt(jnp.mean(x * x, axis=-1, keepdims=True) + EPS)


def _dot(a, b):
    return jnp.dot(a, b, preferred_element_type=_F32)


def _gelu(x):
    half = 0.5 * x
    return half + half * jnp.tanh(x * (_GELU_K1 + _GELU_K2 * (x * x)))


def _silu(x):
    half = 0.5 * x
    return half + half * jnp.tanh(half)


def _resident(shape):
    zeros = (0,) * len(shape)
    return pl.BlockSpec(shape, lambda i, *_: zeros, pipeline_mode=pl.Buffered(1))


def _gmlp_layer_kernel(h_ref, pre_g_ref, wv_ref, wuz_ref, ln_g_ref, ln_b_ref, ws_ref,
                       bs_ref, wout_ref, post_g_ref, *rest, with_kv):
    if with_kv:
        kv_g_ref, wkt_ref, wvp_ref, out_ref, kt_ref, vp_ref, v_scr = rest
    else:
        out_ref, v_scr = rest
    tm = h_ref.shape[0]
    n_chunks = tm // CHUNK
    gw = A_GROUP_CH
    gpd = A_GROUPS_PER_DOT

    h = h_ref[...]
    xn = (h * _rms_scale(h) * pre_g_ref[...]).astype(_BF16)

    n_v = A_WIDTH // V_DOT_WIDTH
    n_d = A_GROUPS // gpd
    v_dot = lambda k: _dot(xn, wv_ref[:, k * V_DOT_WIDTH:(k + 1) * V_DOT_WIDTH])
    uz_dot = lambda d: _dot(xn, wuz_ref[d])
    pending = v_dot(0)
    for k in range(n_v):
        cur = pending
        pending = v_dot(k + 1) if k + 1 < n_v else uz_dot(0)
        v_scr[:, k * V_DOT_WIDTH:(k + 1) * V_DOT_WIDTH] = _gelu(cur)
    mu = jnp.mean(v_scr[...], axis=-1, keepdims=True)
    centered = v_scr[...] - mu
    rstd = lax.rsqrt(jnp.mean(centered * centered, axis=-1, keepdims=True) + EPS)

    row = lax.broadcasted_iota(jnp.int32, (CHUNK, CHUNK), 0)
    col = lax.broadcasted_iota(jnp.int32, (CHUNK, CHUNK), 1)
    causal = col <= row

    acc = None
    for d in range(n_d):
        uz = pending
        if d + 1 < n_d:
            pending = uz_dot(d + 1)
        ys = []
        for j in range(gpd):
            g = d * gpd + j
            cols = slice(g * gw, (g + 1) * gw)
            vn = ((v_scr[:, cols] - mu) * rstd * ln_g_ref[:, cols] + ln_b_ref[:, cols]).astype(_BF16)
            w_causal = jnp.where(causal, ws_ref[g], 0.0).astype(_BF16)
            bias = jnp.concatenate([bs_ref[g]] * (gw // CHUNK), axis=1)
            s = jnp.concatenate(
                [_dot(w_causal, vn[c * CHUNK:(c + 1) * CHUNK]) + bias for c in range(n_chunks)], axis=0)
            u = _gelu(uz[:, j * gw:(j + 1) * gw])
            z = uz[:, (gpd + j) * gw:(gpd + j + 1) * gw]
            ys.append((u * s * _silu(z)).astype(_BF16))
        part = _dot(jnp.concatenate(ys, axis=1), wout_ref[d])
        acc = part if acc is None else acc + part

    h_new = h + acc * _rms_scale(acc) * post_g_ref[...]
    out_ref[...] = h_new

    if with_kv:
        kvn = (h_new * _rms_scale(h_new) * kv_g_ref[...]).astype(_BF16)
        kt = lax.dot_general(wkt_ref[...], kvn, (((1,), (1,)), ((), ())), preferred_element_type=_F32)
        kt_ref[...] = kt.astype(_BF16)
        vp_ref[...] = _dot(kvn, wvp_ref[...]).astype(_BF16)


def _gmlp_layer(h, pre_g, w_in, ln_g, ln_b, w_s, b_s, w_out, post_g, kv=None):
    n_tok = h.shape[0]
    tm = TOKEN_TILE
    n_dots = A_GROUPS // A_GROUPS_PER_DOT
    dot_w = A_GROUPS_PER_DOT * A_GROUP_CH
    wv = w_in[:, A_WIDTH:2 * A_WIDTH].astype(_BF16)
    wu = w_in[:, :A_WIDTH].reshape(D_MODEL, n_dots, dot_w)
    wz = w_in[:, 2 * A_WIDTH:].reshape(D_MODEL, n_dots, dot_w)
    wuz = jnp.concatenate([wu, wz], axis=2).transpose(1, 0, 2).astype(_BF16)
    wout = w_out.reshape(n_dots, dot_w, D_MODEL).astype(_BF16)
    bs = jnp.broadcast_to(b_s[:, :, None], (A_GROUPS, CHUNK, CHUNK)).astype(_F32)

    args = [h, pre_g.reshape(1, D_MODEL), wv, wuz, ln_g.reshape(1, A_WIDTH),
            ln_b.reshape(1, A_WIDTH), w_s, bs, wout, post_g.reshape(1, D_MODEL)]
    tile = pl.BlockSpec((tm, D_MODEL), lambda i: (i, 0))
    in_specs = [tile] + [_resident(a.shape) for a in args[1:]]
    out_shape = [jax.ShapeDtypeStruct((n_tok, D_MODEL), _F32)]
    out_specs = [tile]
    if kv is not None:
        kv_g, w_k, w_v = kv
        extra = [kv_g.reshape(1, D_MODEL), w_k.T.astype(_BF16), w_v.astype(_BF16)]
        args += extra
        in_specs += [_resident(a.shape) for a in extra]
        out_shape += [jax.ShapeDtypeStruct((KV_WIDTH, n_tok), _BF16),
                      jax.ShapeDtypeStruct((n_tok, KV_WIDTH), _BF16)]
        out_specs += [pl.BlockSpec((KV_WIDTH, tm), lambda i: (0, i)),
                      pl.BlockSpec((tm, KV_WIDTH), lambda i: (i, 0))]

    outs = pl.pallas_call(
        functools.partial(_gmlp_layer_kernel, with_kv=kv is not None),
        grid=(n_tok // tm,),
        in_specs=in_specs,
        out_specs=out_specs,
        out_shape=out_shape,
        scratch_shapes=[pltpu.VMEM((tm, A_WIDTH), _F32)],
        compiler_params=pltpu.CompilerParams(
            dimension_semantics=("arbitrary",), vmem_limit_bytes=V7X_VMEM_LIMIT_BYTES),
        name="gmlp_kv_layer" if kv is not None else "gmlp_layer",
    )(*args)
    return outs if kv is not None else outs[0]


def _swa_layer_kernel(sinks_ref, h_ref, pre_g_ref, wqz_ref, ktp_ref, ktc_ref, vpp_ref, vpc_ref,
                      alibi_ref, wout_ref, post_g_ref, out_ref, *, blocks_per_seq):
    tm = h_ref.shape[0]
    n_blocks = tm // CHUNK
    rows = Q_PER_KV * CHUNK

    h = h_ref[...]
    xn = (h * _rms_scale(h) * pre_g_ref[...]).astype(_BF16)
    qz = _dot(xn, wqz_ref[...])
    q = (qz[:, :D_MODEL] * (HEAD_DIM ** -0.5)).astype(_BF16)
    z = qz[:, D_MODEL:]

    kt_all = jnp.concatenate([ktp_ref[...], ktc_ref[...]], axis=1)
    v_all = jnp.concatenate([vpp_ref[...], vpc_ref[...]], axis=0)
    kt_band = lax.broadcasted_iota(jnp.int32, kt_all.shape, 0) // HEAD_DIM
    v_band = lax.broadcasted_iota(jnp.int32, v_all.shape, 1) // HEAD_DIM
    kt_heads = [jnp.where(kt_band == kv, kt_all, jnp.zeros_like(kt_all)) for kv in range(N_KV_HEADS)]
    v_heads = [jnp.where(v_band == kv, v_all, jnp.zeros_like(v_all)) for kv in range(N_KV_HEADS)]

    r = lax.broadcasted_iota(jnp.int32, (CHUNK, CHUNK), 0)
    c = lax.broadcasted_iota(jnp.int32, (CHUNK, CHUNK), 1)
    use_cur = c <= r
    first_in_seq = (pl.program_id(0) * n_blocks) % blocks_per_seq == 0

    o_blocks = []
    for b in range(n_blocks):
        keys = slice(b * CHUNK, (b + 2) * CHUNK)
        tok = slice(b * CHUNK, (b + 1) * CHUNK)
        q_stack = jnp.concatenate([q[tok, g * KV_WIDTH:(g + 1) * KV_WIDTH] for g in range(Q_PER_KV)], axis=0)
        o_stack = None
        for kv in range(N_KV_HEADS):
            scores = _dot(q_stack, kt_heads[kv][:, keys])
            pn = []
            for g in range(Q_PER_KV):
                head = kv * Q_PER_KV + g
                sink = sinks_ref[head]
                s_prev = scores[g * CHUNK:(g + 1) * CHUNK, :CHUNK]
                s_cur = scores[g * CHUNK:(g + 1) * CHUNK, CHUNK:]
                if b == 0:
                    s_prev = jnp.where(first_in_seq, -jnp.inf, s_prev)
                logits = jnp.where(use_cur, s_cur, s_prev) + alibi_ref[head]
                m = jnp.maximum(jnp.max(logits, axis=-1, keepdims=True), sink)
                p = jnp.exp(logits - m)
                denom = jnp.sum(p, axis=-1, keepdims=True) + jnp.exp(sink - m)
                p = p * (1.0 / denom)
                zero = jnp.zeros_like(p)
                pn.append(jnp.concatenate(
                    [jnp.where(use_cur, zero, p), jnp.where(use_cur, p, zero)], axis=1).astype(_BF16))
            part = _dot(jnp.concatenate(pn, axis=0), v_heads[kv][keys, :])
            o_stack = part if o_stack is None else o_stack + part
        o_blocks.append(jnp.concatenate(
            [o_stack[g * CHUNK:(g + 1) * CHUNK] for g in range(Q_PER_KV)], axis=1))
    o = jnp.concatenate(o_blocks, axis=0)

    y = (o * _silu(z)).astype(_BF16)
    out = _dot(y, wout_ref[...])
    out_ref[...] = h + out * _rms_scale(out) * post_g_ref[...]


def _head_permutation():
    g, kv, d = np.meshgrid(np.arange(Q_PER_KV), np.arange(N_KV_HEADS), np.arange(HEAD_DIM), indexing="ij")
    return ((kv * Q_PER_KV + g) * HEAD_DIM + d).reshape(-1)


def _alibi_table():
    r = np.arange(CHUNK)[:, None]
    c = np.arange(CHUNK)[None, :]
    dist = ((r - c) % CHUNK).astype(np.float32)
    slopes = np.array([2.0 ** (-8.0 * (hd + 1) / N_Q_HEADS) for hd in range(N_Q_HEADS)], dtype=np.float32)
    return -slopes[:, None, None] * dist[None]


def _swa_layer(h, kt, vp, pre_g, w_in, sinks, w_out, post_g, seq_len):
    n_tok = h.shape[0]
    tm = TOKEN_TILE
    blocks_per_tile = tm // CHUNK
    perm = _head_permutation()
    wqz = jnp.concatenate([w_in[:, :D_MODEL][:, perm], w_in[:, D_MODEL:][:, perm]], axis=1).astype(_BF16)
    wout = w_out[perm, :].astype(_BF16)
    alibi = jnp.asarray(_alibi_table())

    tile = pl.BlockSpec((tm, D_MODEL), lambda i, s: (i, 0))
    prev_block = lambda i: jnp.maximum(i * blocks_per_tile - 1, 0)
    in_specs = [
        tile,
        _resident((1, D_MODEL)),
        _resident(wqz.shape),
        pl.BlockSpec((KV_WIDTH, CHUNK), lambda i, s: (0, prev_block(i))),
        pl.BlockSpec((KV_WIDTH, tm), lambda i, s: (0, i)),
        pl.BlockSpec((CHUNK, KV_WIDTH), lambda i, s: (prev_block(i), 0)),
        pl.BlockSpec((tm, KV_WIDTH), lambda i, s: (i, 0)),
        _resident(alibi.shape),
        _resident(wout.shape),
        _resident((1, D_MODEL)),
    ]
    return pl.pallas_call(
        functools.partial(_swa_layer_kernel, blocks_per_seq=seq_len // CHUNK),
        grid_spec=pltpu.PrefetchScalarGridSpec(
            num_scalar_prefetch=1,
            grid=(n_tok // tm,),
            in_specs=in_specs,
            out_specs=tile,
        ),
        out_shape=jax.ShapeDtypeStruct((n_tok, D_MODEL), _F32),
        compiler_params=pltpu.CompilerParams(
            dimension_semantics=("arbitrary",), vmem_limit_bytes=V7X_VMEM_LIMIT_BYTES),
        name="swa_layer",
    )(sinks.astype(_F32), h, pre_g.reshape(1, D_MODEL), wqz, kt, kt, vp, vp, alibi, wout,
      post_g.reshape(1, D_MODEL))


def kernel(x, a_w_in, a_ln_g, a_ln_b, a_w_s, a_b_s, a_w_out, a_pre_g, a_post_g, kv_norm_g, w_k, w_v,
           b_w_in, b_sinks, b_w_out, b_pre_g, b_post_g):
    bsz, seq_len, d_model = x.shape
    assert d_model == D_MODEL and seq_len % TOKEN_TILE == 0 and TOKEN_TILE % CHUNK == 0
    n_a = a_w_in.shape[0]
    h = x.reshape(bsz * seq_len, d_model)
    kt = vp = None
    for i in range(n_a):
        kv = (kv_norm_g, w_k, w_v) if i == n_a - 1 else None
        res = _gmlp_layer(h, a_pre_g[i], a_w_in[i], a_ln_g[i], a_ln_b[i], a_w_s[i], a_b_s[i],
                          a_w_out[i], a_post_g[i], kv=kv)
        if kv is None:
            h = res
        else:
            h, kt, vp = res
    for j in range(b_w_in.shape[0]):
        h = _swa_layer(h, kt, vp, b_pre_g[j], b_w_in[j], b_sinks[j], b_w_out[j], b_post_g[j], seq_len)
    return h.reshape(bsz, seq_len, d_model)
```

```python
import functools

import numpy as np
import jax
import jax.numpy as jnp
from jax import lax
from jax.experimental import pallas as pl
from jax.experimental.pallas import tpu as pltpu

D_MODEL = 1024
CHUNK = 128
A_WIDTH = 2 * D_MODEL
A_GROUPS = 8
A_GROUP_CH = A_WIDTH // A_GROUPS
A_GROUPS_PER_DOT = 2
V_DOT_WIDTH = 1024
PROJ_PIECE_WIDTH = 256
TAIL_PIECES = 0
HEAD_DIM = 64
N_Q_HEADS = 16
N_KV_HEADS = 4
Q_PER_KV = N_Q_HEADS // N_KV_HEADS
KV_WIDTH = N_KV_HEADS * HEAD_DIM
EPS = 1e-6

TOKEN_TILE = 512
V7X_VMEM_LIMIT_BYTES = 56 * 1024 * 1024

_BF16 = jnp.bfloat16
_F32 = jnp.float32

_GELU_K1 = 0.7978845608028654
_GELU_K2 = _GELU_K1 * 0.044715


def _rms_scale(x):
    return lax.rsqrt(jnp.mean(x * x, axis=-1, keepdims=True) + EPS)


def _dot(a, b):
    return jnp.dot(a, b, preferred_element_type=_F32)


def _gelu(x):
    half = 0.5 * x
    return half + half * jnp.tanh(x * (_GELU_K1 + _GELU_K2 * (x * x)))


def _silu(x):
    half = 0.5 * x
    return half + half * jnp.tanh(half)


def _resident(shape):
    zeros = (0,) * len(shape)
    return pl.BlockSpec(shape, lambda i, *_: zeros, pipeline_mode=pl.Buffered(1))


def _gmlp_layer_kernel(h_ref, pre_g_ref, wv_ref, wuz_ref, ln_g_ref, ln_b_ref, ws_ref,
                       bs_ref, wout_ref, post_g_ref, *rest, with_kv):
    if with_kv:
        kv_g_ref, wkt_ref, wvp_ref, out_ref, kt_ref, vp_ref, v_scr = rest
    else:
        out_ref, v_scr = rest
    tm = h_ref.shape[0]
    n_chunks = tm // CHUNK
    gw = A_GROUP_CH
    gpd = A_GROUPS_PER_DOT

    h = h_ref[...]
    xn = (h * _rms_scale(h) * pre_g_ref[...]).astype(_BF16)

    n_v = A_WIDTH // V_DOT_WIDTH
    n_d = A_GROUPS // gpd
    v_dot = lambda k: _dot(xn, wv_ref[:, k * V_DOT_WIDTH:(k + 1) * V_DOT_WIDTH])
    uz_dot = lambda d: _dot(xn, wuz_ref[d])
    pending = v_dot(0)
    for k in range(n_v):
        cur = pending
        pending = v_dot(k + 1) if k + 1 < n_v else uz_dot(0)
        v_scr[:, k * V_DOT_WIDTH:(k + 1) * V_DOT_WIDTH] = _gelu(cur)
    mu = jnp.mean(v_scr[...], axis=-1, keepdims=True)
    centered = v_scr[...] - mu
    rstd = lax.rsqrt(jnp.mean(centered * centered, axis=-1, keepdims=True) + EPS)

    row = lax.broadcasted_iota(jnp.int32, (CHUNK, CHUNK), 0)
    col = lax.broadcasted_iota(jnp.int32, (CHUNK, CHUNK), 1)
    causal = col <= row

    acc = None
    for d in range(n_d):
        uz = pending
        if d + 1 < n_d:
            pending = uz_dot(d + 1)
        ys = []
        for j in range(gpd):
            g = d * gpd + j
            cols = slice(g * gw, (g + 1) * gw)
            vn = ((v_scr[:, cols] - mu) * rstd * ln_g_ref[:, cols] + ln_b_ref[:, cols]).astype(_BF16)
            w_causal = jnp.where(causal, ws_ref[g], 0.0).astype(_BF16)
            bias = jnp.concatenate([bs_ref[g]] * (gw // CHUNK), axis=1)
            s = jnp.concatenate(
                [_dot(w_causal, vn[c * CHUNK:(c + 1) * CHUNK]) + bias for c in range(n_chunks)], axis=0)
            u = _gelu(uz[:, j * gw:(j + 1) * gw])
            z = uz[:, (gpd + j) * gw:(gpd + j + 1) * gw]
            ys.append((u * s * _silu(z)).astype(_BF16))
        part = _dot(jnp.concatenate(ys, axis=1), wout_ref[d])
        acc = part if acc is None else acc + part

    h_new = h + acc * _rms_scale(acc) * post_g_ref[...]
    out_ref[...] = h_new

    if with_kv:
        kvn = (h_new * _rms_scale(h_new) * kv_g_ref[...]).astype(_BF16)
        kt = lax.dot_general(wkt_ref[...], kvn, (((1,), (1,)), ((), ())), preferred_element_type=_F32)
        kt_ref[...] = kt.astype(_BF16)
        vp_ref[...] = _dot(kvn, wvp_ref[...]).astype(_BF16)


def _gmlp_layer(h, pre_g, w_in, ln_g, ln_b, w_s, b_s, w_out, post_g, kv=None):
    n_tok = h.shape[0]
    tm = TOKEN_TILE
    n_dots = A_GROUPS // A_GROUPS_PER_DOT
    dot_w = A_GROUPS_PER_DOT * A_GROUP_CH
    wv = w_in[:, A_WIDTH:2 * A_WIDTH].astype(_BF16)
    wu = w_in[:, :A_WIDTH].reshape(D_MODEL, n_dots, dot_w)
    wz = w_in[:, 2 * A_WIDTH:].reshape(D_MODEL, n_dots, dot_w)
    wuz = jnp.concatenate([wu, wz], axis=2).transpose(1, 0, 2).astype(_BF16)
    wout = w_out.reshape(n_dots, dot_w, D_MODEL).astype(_BF16)
    bs = jnp.broadcast_to(b_s[:, :, None], (A_GROUPS, CHUNK, CHUNK)).astype(_F32)

    args = [h, pre_g.reshape(1, D_MODEL), wv, wuz, ln_g.reshape(1, A_WIDTH),
            ln_b.reshape(1, A_WIDTH), w_s, bs, wout, post_g.reshape(1, D_MODEL)]
    tile = pl.BlockSpec((tm, D_MODEL), lambda i: (i, 0))
    in_specs = [tile] + [_resident(a.shape) for a in args[1:]]
    out_shape = [jax.ShapeDtypeStruct((n_tok, D_MODEL), _F32)]
    out_specs = [tile]
    if kv is not None:
        kv_g, w_k, w_v = kv
        extra = [kv_g.reshape(1, D_MODEL), w_k.T.astype(_BF16), w_v.astype(_BF16)]
        args += extra
        in_specs += [_resident(a.shape) for a in extra]
        out_shape += [jax.ShapeDtypeStruct((KV_WIDTH, n_tok), _BF16),
                      jax.ShapeDtypeStruct((n_tok, KV_WIDTH), _BF16)]
        out_specs += [pl.BlockSpec((KV_WIDTH, tm), lambda i: (0, i)),
                      pl.BlockSpec((tm, KV_WIDTH), lambda i: (i, 0))]

    outs = pl.pallas_call(
        functools.partial(_gmlp_layer_kernel, with_kv=kv is not None),
        grid=(n_tok // tm,),
        in_specs=in_specs,
        out_specs=out_specs,
        out_shape=out_shape,
        scratch_shapes=[pltpu.VMEM((tm, A_WIDTH), _F32)],
        compiler_params=pltpu.CompilerParams(
            dimension_semantics=("arbitrary",), vmem_limit_bytes=V7X_VMEM_LIMIT_BYTES),
        name="gmlp_kv_layer" if kv is not None else "gmlp_layer",
    )(*args)
    return outs if kv is not None else outs[0]


def _swa_layer_kernel(h_ref, pre_g_ref, wqz_ref, ktp_ref, ktc_ref, vpp_ref, vpc_ref, alibi_ref,
                      sink_ref, wout_ref, post_g_ref, out_ref, q_scr, z_scr, h_scr, *, blocks_per_seq):
    step = pl.program_id(0)
    tm = h_ref.shape[0]
    n_blocks = tm // CHUNK
    n_pieces = 2 * D_MODEL // PROJ_PIECE_WIDTH
    units = [(b, kv) for b in range(n_blocks) for kv in range(N_KV_HEADS)]

    @pl.when(step == 0)
    def _():
        q_scr[...] = jnp.zeros_like(q_scr)
        z_scr[...] = jnp.zeros_like(z_scr)
        h_scr[...] = jnp.zeros_like(h_scr)

    kt_all = jnp.concatenate([ktp_ref[...], ktc_ref[...]], axis=1)
    v_all = jnp.concatenate([vpp_ref[...], vpc_ref[...]], axis=0)
    kt_band = lax.broadcasted_iota(jnp.int32, kt_all.shape, 0) // HEAD_DIM
    v_band = lax.broadcasted_iota(jnp.int32, v_all.shape, 1) // HEAD_DIM
    kt_heads = [jnp.where(kt_band == kv, kt_all, jnp.zeros_like(kt_all)) for kv in range(N_KV_HEADS)]
    v_heads = [jnp.where(v_band == kv, v_all, jnp.zeros_like(v_all)) for kv in range(N_KV_HEADS)]

    rows = Q_PER_KV * CHUNK
    r = lax.broadcasted_iota(jnp.int32, (rows, CHUNK), 0) % CHUNK
    c = lax.broadcasted_iota(jnp.int32, (rows, CHUNK), 1)
    use_cur = c <= r
    first_in_seq = (jnp.maximum(step - 1, 0) * n_blocks) % blocks_per_seq == 0

    def block_scores(b):
        keys = slice(b * CHUNK, (b + 2) * CHUNK)
        tok = slice(b * CHUNK, (b + 1) * CHUNK)
        q_stack = jnp.concatenate(
            [q_scr[tok, g * KV_WIDTH:(g + 1) * KV_WIDTH] for g in range(Q_PER_KV)], axis=0)
        return _dot(q_stack, jnp.concatenate([kt_heads[kv][:, keys] for kv in range(N_KV_HEADS)], axis=1))

    def unit_probs(b, kv, scores):
        s_prev = scores[:, 2 * kv * CHUNK:(2 * kv + 1) * CHUNK]
        s_cur = scores[:, (2 * kv + 1) * CHUNK:(2 * kv + 2) * CHUNK]
        if b == 0:
            s_prev = jnp.where(first_in_seq, -jnp.inf, s_prev)
        sink = sink_ref[kv]
        logits = jnp.where(use_cur, s_cur, s_prev) + alibi_ref[kv]
        m = jnp.maximum(jnp.max(logits, axis=-1, keepdims=True), sink)
        p = jnp.exp(logits - m)
        denom = jnp.sum(p, axis=-1, keepdims=True) + jnp.exp(sink - m)
        p = p * (1.0 / denom)
        zero = jnp.zeros_like(p)
        return jnp.concatenate([jnp.where(use_cur, zero, p), jnp.where(use_cur, p, zero)],
                               axis=1).astype(_BF16)

    def block_output(b, probs):
        keys = slice(b * CHUNK, (b + 2) * CHUNK)
        values = jnp.concatenate([v_heads[kv][keys, :] for kv in range(N_KV_HEADS)], axis=0)
        return _dot(jnp.concatenate(probs, axis=1), values)

    pieces = []
    proj_piece = lambda: pieces.append(
        _dot(xn, wqz_ref[:, len(pieces) * PROJ_PIECE_WIDTH:(len(pieces) + 1) * PROJ_PIECE_WIDTH]))
    piece_at = {(k * len(units)) // (n_pieces - TAIL_PIECES) for k in range(n_pieces - TAIL_PIECES)}
    o_stacks = []
    scores_next = block_scores(0)
    h_next = h_ref[...]
    xn = (h_next * _rms_scale(h_next) * pre_g_ref[...]).astype(_BF16)
    for b in range(n_blocks):
        scores = scores_next
        if b + 1 < n_blocks:
            scores_next = block_scores(b + 1)
        probs = []
        for kv in range(N_KV_HEADS):
            if b * N_KV_HEADS + kv in piece_at:
                proj_piece()
            probs.append(unit_probs(b, kv, scores))
        o_stacks.append(block_output(b, probs))
    o = jnp.concatenate(
        [jnp.concatenate([o_stack[g * CHUNK:(g + 1) * CHUNK] for g in range(Q_PER_KV)], axis=1)
         for o_stack in o_stacks], axis=0)

    y = (o * _silu(z_scr[...])).astype(_BF16)
    out = _dot(y, wout_ref[...])
    for _ in range(TAIL_PIECES):
        proj_piece()
    out_ref[...] = h_scr[...] + out * _rms_scale(out) * post_g_ref[...]

    qz = jnp.concatenate(pieces, axis=1)
    q_scr[...] = (qz[:, :D_MODEL] * (HEAD_DIM ** -0.5)).astype(_BF16)
    z_scr[...] = qz[:, D_MODEL:]
    h_scr[...] = h_next


def _head_permutation():
    g, kv, d = np.meshgrid(np.arange(Q_PER_KV), np.arange(N_KV_HEADS), np.arange(HEAD_DIM), indexing="ij")
    return ((kv * Q_PER_KV + g) * HEAD_DIM + d).reshape(-1)


def _alibi_table():
    r = np.arange(CHUNK)[:, None]
    c = np.arange(CHUNK)[None, :]
    dist = ((r - c) % CHUNK).astype(np.float32)
    slopes = np.array([2.0 ** (-8.0 * (hd + 1) / N_Q_HEADS) for hd in range(N_Q_HEADS)], dtype=np.float32)
    per_head = -slopes[:, None, None] * dist[None]
    return per_head.reshape(N_KV_HEADS, Q_PER_KV * CHUNK, CHUNK)


def _swa_layer(h, kt, vp, pre_g, w_in, sinks, w_out, post_g, seq_len):
    n_tok = h.shape[0]
    tm = TOKEN_TILE
    blocks_per_tile = tm // CHUNK
    perm = _head_permutation()
    wqz = jnp.concatenate([w_in[:, :D_MODEL][:, perm], w_in[:, D_MODEL:][:, perm]], axis=1).astype(_BF16)
    wout = w_out[perm, :].astype(_BF16)
    alibi = jnp.asarray(_alibi_table())
    sink_tbl = jnp.broadcast_to(sinks.astype(_F32).reshape(N_KV_HEADS, Q_PER_KV, 1, 1),
                                (N_KV_HEADS, Q_PER_KV, CHUNK, CHUNK)).reshape(alibi.shape)

    n_tiles = n_tok // tm
    proj_tile = lambda i: jnp.minimum(i, n_tiles - 1)
    attn_tile = lambda i: jnp.maximum(i - 1, 0)
    prev_block = lambda i: jnp.maximum(attn_tile(i) * blocks_per_tile - 1, 0)
    in_specs = [
        pl.BlockSpec((tm, D_MODEL), lambda i: (proj_tile(i), 0)),
        _resident((1, D_MODEL)),
        _resident(wqz.shape),
        pl.BlockSpec((KV_WIDTH, CHUNK), lambda i: (0, prev_block(i))),
        pl.BlockSpec((KV_WIDTH, tm), lambda i: (0, attn_tile(i))),
        pl.BlockSpec((CHUNK, KV_WIDTH), lambda i: (prev_block(i), 0)),
        pl.BlockSpec((tm, KV_WIDTH), lambda i: (attn_tile(i), 0)),
        _resident(alibi.shape),
        _resident(sink_tbl.shape),
        _resident(wout.shape),
        _resident((1, D_MODEL)),
    ]
    return pl.pallas_call(
        functools.partial(_swa_layer_kernel, blocks_per_seq=seq_len // CHUNK),
        grid=(n_tiles + 1,),
        in_specs=in_specs,
        out_specs=pl.BlockSpec((tm, D_MODEL), lambda i: (attn_tile(i), 0)),
        scratch_shapes=[pltpu.VMEM((tm, D_MODEL), _BF16), pltpu.VMEM((tm, D_MODEL), _F32),
                        pltpu.VMEM((tm, D_MODEL), _F32)],
        out_shape=jax.ShapeDtypeStruct((n_tok, D_MODEL), _F32),
        compiler_params=pltpu.CompilerParams(
            dimension_semantics=("arbitrary",), vmem_limit_bytes=V7X_VMEM_LIMIT_BYTES),
        name="swa_layer",
    )(h, pre_g.reshape(1, D_MODEL), wqz, kt, kt, vp, vp, alibi, sink_tbl, wout,
      post_g.reshape(1, D_MODEL))


def kernel(x, a_w_in, a_ln_g, a_ln_b, a_w_s, a_b_s, a_w_out, a_pre_g, a_post_g, kv_norm_g, w_k, w_v,
           b_w_in, b_sinks, b_w_out, b_pre_g, b_post_g):
    bsz, seq_len, d_model = x.shape
    assert d_model == D_MODEL and seq_len % TOKEN_TILE == 0 and TOKEN_TILE % CHUNK == 0
    n_a = a_w_in.shape[0]
    h = x.reshape(bsz * seq_len, d_model)
    kt = vp = None
    for i in range(n_a):
        kv = (kv_norm_g, w_k, w_v) if i == n_a - 1 else None
        res = _gmlp_layer(h, a_pre_g[i], a_w_in[i], a_ln_g[i], a_ln_b[i], a_w_s[i], a_b_s[i],
                          a_w_out[i], a_post_g[i], kv=kv)
        if kv is None:
            h = res
        else:
            h, kt, vp = res
    for j in range(b_w_in.shape[0]):
        h = _swa_layer(h, kt, vp, b_pre_g[j], b_w_in[j], b_sinks[j], b_w_out[j], b_post_g[j], seq_len)
    return h.reshape(bsz, seq_len, d_model)
```

```python
import functools

import numpy as np
import jax
import jax.numpy as jnp
from jax import lax
from jax.experimental import pallas as pl
from jax.experimental.pallas import tpu as pltpu

D_MODEL = 1024
CHUNK = 128
A_WIDTH = 2 * D_MODEL
A_GROUPS = 8
A_GROUP_CH = A_WIDTH // A_GROUPS
A_GROUPS_PER_DOT = 2
V_DOT_WIDTH = 1024
PROJ_PIECE_WIDTH = 256
HEAD_DIM = 64
N_Q_HEADS = 16
N_KV_HEADS = 4
Q_PER_KV = N_Q_HEADS // N_KV_HEADS
KV_WIDTH = N_KV_HEADS * HEAD_DIM
EPS = 1e-6

TOKEN_TILE = 512
V7X_VMEM_LIMIT_BYTES = 56 * 1024 * 1024

_BF16 = jnp.bfloat16
_F32 = jnp.float32

_GELU_K1 = 0.7978845608028654
_GELU_K2 = _GELU_K1 * 0.044715


def _rms_scale(x):
    return lax.rsqrt(jnp.mean(x * x, axis=-1, keepdims=True) + EPS)


def _dot(a, b):
    return jnp.dot(a, b, preferred_element_type=_F32)


def _gelu(x):
    half = 0.5 * x
    return half + half * jnp.tanh(x * (_GELU_K1 + _GELU_K2 * (x * x)))


def _silu(x):
    half = 0.5 * x
    return half + half * jnp.tanh(half)


def _resident(shape):
    zeros = (0,) * len(shape)
    return pl.BlockSpec(shape, lambda i, *_: zeros, pipeline_mode=pl.Buffered(1))


def _gmlp_layer_kernel(h_ref, pre_g_ref, wv_ref, wuz_ref, ln_g_ref, ln_b_ref, ws_ref,
                       bs_ref, wout_ref, post_g_ref, *rest, with_kv):
    if with_kv:
        kv_g_ref, wkt_ref, wvp_ref, out_ref, kt_ref, vp_ref, v_scr = rest
    else:
        out_ref, v_scr = rest
    tm = h_ref.shape[0]
    n_chunks = tm // CHUNK
    gw = A_GROUP_CH
    gpd = A_GROUPS_PER_DOT
    pair_w = gpd * gw

    h = h_ref[...]
    xn = (h * _rms_scale(h) * pre_g_ref[...]).astype(_BF16)

    n_v = A_WIDTH // V_DOT_WIDTH
    n_d = A_GROUPS // gpd
    v_dot = lambda k: _dot(xn, wv_ref[:, k * V_DOT_WIDTH:(k + 1) * V_DOT_WIDTH])
    uz_dot = lambda d: _dot(xn, wuz_ref[:, d * 2 * pair_w:(d + 1) * 2 * pair_w])
    pending = v_dot(0)
    for k in range(n_v):
        cur = pending
        pending = v_dot(k + 1) if k + 1 < n_v else uz_dot(0)
        v_scr[:, k * V_DOT_WIDTH:(k + 1) * V_DOT_WIDTH] = _gelu(cur)
    mu = jnp.mean(v_scr[...], axis=-1, keepdims=True)
    centered = v_scr[...] - mu
    rstd = lax.rsqrt(jnp.mean(centered * centered, axis=-1, keepdims=True) + EPS)

    row = lax.broadcasted_iota(jnp.int32, (CHUNK, CHUNK), 0)
    col = lax.broadcasted_iota(jnp.int32, (CHUNK, CHUNK), 1)
    causal = col <= row

    acc = None
    for d in range(n_d):
        uz = pending
        if d + 1 < n_d:
            pending = uz_dot(d + 1)
        ys = []
        for j in range(gpd):
            g = d * gpd + j
            cols = slice(g * gw, (g + 1) * gw)
            vn = ((v_scr[:, cols] - mu) * rstd * ln_g_ref[:, cols] + ln_b_ref[:, cols]).astype(_BF16)
            w_causal = jnp.where(causal, ws_ref[g], 0.0).astype(_BF16)
            bias = jnp.concatenate([bs_ref[g]] * (gw // CHUNK), axis=1)
            s = jnp.concatenate(
                [_dot(w_causal, vn[c * CHUNK:(c + 1) * CHUNK]) + bias for c in range(n_chunks)], axis=0)
            u = _gelu(uz[:, j * gw:(j + 1) * gw])
            z = uz[:, (gpd + j) * gw:(gpd + j + 1) * gw]
            ys.append((u * s * _silu(z)).astype(_BF16))
        part = _dot(jnp.concatenate(ys, axis=1), wout_ref[d * pair_w:(d + 1) * pair_w, :])
        acc = part if acc is None else acc + part

    h_new = h + acc * _rms_scale(acc) * post_g_ref[...]
    out_ref[...] = h_new

    if with_kv:
        kvn = (h_new * _rms_scale(h_new) * kv_g_ref[...]).astype(_BF16)
        kt = lax.dot_general(wkt_ref[...], kvn, (((1,), (1,)), ((), ())), preferred_element_type=_F32)
        kt_ref[...] = kt.astype(_BF16)
        vp_ref[...] = _dot(kvn, wvp_ref[...]).astype(_BF16)


def _gmlp_layer(h, pre_g, w_in, ln_g, ln_b, w_s, b_s, w_out, post_g, kv=None):
    n_tok = h.shape[0]
    tm = TOKEN_TILE
    n_dots = A_GROUPS // A_GROUPS_PER_DOT
    dot_w = A_GROUPS_PER_DOT * A_GROUP_CH
    wv = w_in[:, A_WIDTH:2 * A_WIDTH].astype(_BF16)
    wuz = jnp.concatenate(
        [w_in[:, base + d * dot_w:base + (d + 1) * dot_w] for d in range(n_dots) for base in (0, 2 * A_WIDTH)],
        axis=1).astype(_BF16)
    wout = w_out.astype(_BF16)
    bs = jnp.broadcast_to(b_s[:, :, None], (A_GROUPS, CHUNK, CHUNK)).astype(_F32)

    args = [h, pre_g.reshape(1, D_MODEL), wv, wuz, ln_g.reshape(1, A_WIDTH),
            ln_b.reshape(1, A_WIDTH), w_s, bs, wout, post_g.reshape(1, D_MODEL)]
    tile = pl.BlockSpec((tm, D_MODEL), lambda i: (i, 0))
    in_specs = [tile] + [_resident(a.shape) for a in args[1:]]
    out_shape = [jax.ShapeDtypeStruct((n_tok, D_MODEL), _F32)]
    out_specs = [tile]
    if kv is not None:
        kv_g, w_k, w_v = kv
        extra = [kv_g.reshape(1, D_MODEL), w_k.T.astype(_BF16), w_v.astype(_BF16)]
        args += extra
        in_specs += [_resident(a.shape) for a in extra]
        out_shape += [jax.ShapeDtypeStruct((KV_WIDTH, n_tok), _BF16),
                      jax.ShapeDtypeStruct((n_tok, KV_WIDTH), _BF16)]
        out_specs += [pl.BlockSpec((KV_WIDTH, tm), lambda i: (0, i)),
                      pl.BlockSpec((tm, KV_WIDTH), lambda i: (i, 0))]

    outs = pl.pallas_call(
        functools.partial(_gmlp_layer_kernel, with_kv=kv is not None),
        grid=(n_tok // tm,),
        in_specs=in_specs,
        out_specs=out_specs,
        out_shape=out_shape,
        scratch_shapes=[pltpu.VMEM((tm, A_WIDTH), _F32)],
        compiler_params=pltpu.CompilerParams(
            dimension_semantics=("arbitrary",), vmem_limit_bytes=V7X_VMEM_LIMIT_BYTES),
        name="gmlp_kv_layer" if kv is not None else "gmlp_layer",
    )(*args)
    return outs if kv is not None else outs[0]


def _swa_layer_kernel(h_ref, pre_g_ref, wqz_ref, ktp_ref, ktc_ref, vpp_ref, vpc_ref, alibi_ref,
                      sink_ref, wout_ref, post_g_ref, out_ref, q_scr, z_scr, h_scr, *, blocks_per_seq):
    step = pl.program_id(0)
    tm = h_ref.shape[0]
    n_blocks = tm // CHUNK
    n_pieces = 2 * D_MODEL // PROJ_PIECE_WIDTH
    n_units = n_blocks * N_KV_HEADS

    @pl.when(step == 0)
    def _():
        q_scr[...] = jnp.zeros_like(q_scr)
        z_scr[...] = jnp.zeros_like(z_scr)
        h_scr[...] = jnp.zeros_like(h_scr)

    kt_all = jnp.concatenate([ktp_ref[...], ktc_ref[...]], axis=1)
    v_all = jnp.concatenate([vpp_ref[...], vpc_ref[...]], axis=0)
    kt_band = lax.broadcasted_iota(jnp.int32, kt_all.shape, 0) // HEAD_DIM
    v_band = lax.broadcasted_iota(jnp.int32, v_all.shape, 1) // HEAD_DIM
    kt_heads = [jnp.where(kt_band == kv, kt_all, jnp.zeros_like(kt_all)) for kv in range(N_KV_HEADS)]
    v_heads = [jnp.where(v_band == kv, v_all, jnp.zeros_like(v_all)) for kv in range(N_KV_HEADS)]

    rows = Q_PER_KV * CHUNK
    r = lax.broadcasted_iota(jnp.int32, (rows, CHUNK), 0) % CHUNK
    c = lax.broadcasted_iota(jnp.int32, (rows, CHUNK), 1)
    use_cur = c <= r
    first_in_seq = (jnp.maximum(step - 1, 0) * n_blocks) % blocks_per_seq == 0

    def block_scores(b):
        keys = slice(b * CHUNK, (b + 2) * CHUNK)
        tok = slice(b * CHUNK, (b + 1) * CHUNK)
        q_stack = jnp.concatenate(
            [q_scr[tok, g * KV_WIDTH:(g + 1) * KV_WIDTH] for g in range(Q_PER_KV)], axis=0)
        return _dot(q_stack, jnp.concatenate([kt_heads[kv][:, keys] for kv in range(N_KV_HEADS)], axis=1))

    def unit_probs(b, kv, scores):
        s_prev = scores[:, 2 * kv * CHUNK:(2 * kv + 1) * CHUNK]
        s_cur = scores[:, (2 * kv + 1) * CHUNK:(2 * kv + 2) * CHUNK]
        if b == 0:
            s_prev = jnp.where(first_in_seq, -jnp.inf, s_prev)
        sink = sink_ref[kv]
        logits = jnp.where(use_cur, s_cur, s_prev) + alibi_ref[kv]
        m = jnp.maximum(jnp.max(logits, axis=-1, keepdims=True), sink)
        p = jnp.exp(logits - m)
        denom = jnp.sum(p, axis=-1, keepdims=True) + jnp.exp(sink - m)
        p = p * (1.0 / denom)
        zero = jnp.zeros_like(p)
        return jnp.concatenate([jnp.where(use_cur, zero, p), jnp.where(use_cur, p, zero)],
                               axis=1).astype(_BF16)

    def block_output(b, probs):
        keys = slice(b * CHUNK, (b + 2) * CHUNK)
        values = jnp.concatenate([v_heads[kv][keys, :] for kv in range(N_KV_HEADS)], axis=0)
        return _dot(jnp.concatenate(probs, axis=1), values)

    pieces = []
    proj_piece = lambda: pieces.append(
        _dot(xn, wqz_ref[:, len(pieces) * PROJ_PIECE_WIDTH:(len(pieces) + 1) * PROJ_PIECE_WIDTH]))
    piece_at = {(k * n_units) // n_pieces for k in range(n_pieces)}
    o_stacks = []
    scores_next = block_scores(0)
    h_next = h_ref[...]
    xn = (h_next * _rms_scale(h_next) * pre_g_ref[...]).astype(_BF16)
    for b in range(n_blocks):
        scores = scores_next
        if b + 1 < n_blocks:
            scores_next = block_scores(b + 1)
        probs = []
        for kv in range(N_KV_HEADS):
            if b * N_KV_HEADS + kv in piece_at:
                proj_piece()
            probs.append(unit_probs(b, kv, scores))
        o_stacks.append(block_output(b, probs))
    o = jnp.concatenate(
        [jnp.concatenate([o_stack[g * CHUNK:(g + 1) * CHUNK] for g in range(Q_PER_KV)], axis=1)
         for o_stack in o_stacks], axis=0)

    y = (o * _silu(z_scr[...])).astype(_BF16)
    out = _dot(y, wout_ref[...])
    out_ref[...] = h_scr[...] + out * _rms_scale(out) * post_g_ref[...]

    qz = jnp.concatenate(pieces, axis=1)
    q_scr[...] = (qz[:, :D_MODEL] * (HEAD_DIM ** -0.5)).astype(_BF16)
    z_scr[...] = qz[:, D_MODEL:]
    h_scr[...] = h_next


def _heads_to_g_major(w, axis):
    lead = w.shape[:axis]
    tail = w.shape[axis + 1:]
    w = w.reshape(lead + (N_KV_HEADS, Q_PER_KV, HEAD_DIM) + tail)
    return jnp.swapaxes(w, axis, axis + 1).reshape(lead + (N_Q_HEADS * HEAD_DIM,) + tail)


def _alibi_table():
    r = np.arange(CHUNK)[:, None]
    c = np.arange(CHUNK)[None, :]
    dist = ((r - c) % CHUNK).astype(np.float32)
    slopes = np.array([2.0 ** (-8.0 * (hd + 1) / N_Q_HEADS) for hd in range(N_Q_HEADS)], dtype=np.float32)
    per_head = -slopes[:, None, None] * dist[None]
    return per_head.reshape(N_KV_HEADS, Q_PER_KV * CHUNK, CHUNK)


def _swa_layer(h, kt, vp, pre_g, w_in, sinks, w_out, post_g, seq_len):
    n_tok = h.shape[0]
    tm = TOKEN_TILE
    blocks_per_tile = tm // CHUNK
    wqz = jnp.concatenate([_heads_to_g_major(w_in[:, :D_MODEL], 1), _heads_to_g_major(w_in[:, D_MODEL:], 1)],
                          axis=1).astype(_BF16)
    wout = _heads_to_g_major(w_out, 0).astype(_BF16)
    alibi = jnp.asarray(_alibi_table())
    sink_tbl = jnp.broadcast_to(sinks.astype(_F32).reshape(N_KV_HEADS, Q_PER_KV, 1, 1),
                                (N_KV_HEADS, Q_PER_KV, CHUNK, CHUNK)).reshape(alibi.shape)

    n_tiles = n_tok // tm
    proj_tile = lambda i: jnp.minimum(i, n_tiles - 1)
    attn_tile = lambda i: jnp.maximum(i - 1, 0)
    prev_block = lambda i: jnp.maximum(attn_tile(i) * blocks_per_tile - 1, 0)
    in_specs = [
        pl.BlockSpec((tm, D_MODEL), lambda i: (proj_tile(i), 0)),
        _resident((1, D_MODEL)),
        _resident(wqz.shape),
        pl.BlockSpec((KV_WIDTH, CHUNK), lambda i: (0, prev_block(i))),
        pl.BlockSpec((KV_WIDTH, tm), lambda i: (0, attn_tile(i))),
        pl.BlockSpec((CHUNK, KV_WIDTH), lambda i: (prev_block(i), 0)),
        pl.BlockSpec((tm, KV_WIDTH), lambda i: (attn_tile(i), 0)),
        _resident(alibi.shape),
        _resident(sink_tbl.shape),
        _resident(wout.shape),
        _resident((1, D_MODEL)),
    ]
    return pl.pallas_call(
        functools.partial(_swa_layer_kernel, blocks_per_seq=seq_len // CHUNK),
        grid=(n_tiles + 1,),
        in_specs=in_specs,
        out_specs=pl.BlockSpec((tm, D_MODEL), lambda i: (attn_tile(i), 0)),
        scratch_shapes=[pltpu.VMEM((tm, D_MODEL), _BF16), pltpu.VMEM((tm, D_MODEL), _F32),
                        pltpu.VMEM((tm, D_MODEL), _F32)],
        out_shape=jax.ShapeDtypeStruct((n_tok, D_MODEL), _F32),
        compiler_params=pltpu.CompilerParams(
            dimension_semantics=("arbitrary",), vmem_limit_bytes=V7X_VMEM_LIMIT_BYTES),
        name="swa_layer",
    )(h, pre_g.reshape(1, D_MODEL), wqz, kt, kt, vp, vp, alibi, sink_tbl, wout,
      post_g.reshape(1, D_MODEL))


def kernel(x, a_w_in, a_ln_g, a_ln_b, a_w_s, a_b_s, a_w_out, a_pre_g, a_post_g, kv_norm_g, w_k, w_v,
           b_w_in, b_sinks, b_w_out, b_pre_g, b_post_g):
    bsz, seq_len, d_model = x.shape
    assert d_model == D_MODEL and seq_len % TOKEN_TILE == 0 and TOKEN_TILE % CHUNK == 0
    n_a = a_w_in.shape[0]
    h = x.reshape(bsz * seq_len, d_model)
    kt = vp = None
    for i in range(n_a):
        kv = (kv_norm_g, w_k, w_v) if i == n_a - 1 else None
        res = _gmlp_layer(h, a_pre_g[i], a_w_in[i], a_ln_g[i], a_ln_b[i], a_w_s[i], a_b_s[i],
                          a_w_out[i], a_post_g[i], kv=kv)
        if kv is None:
            h = res
        else:
            h, kt, vp = res
    for j in range(b_w_in.shape[0]):
        h = _swa_layer(h, kt, vp, b_pre_g[j], b_w_in[j], b_sinks[j], b_w_out[j], b_post_g[j], seq_len)
    return h.reshape(bsz, seq_len, d_model)
```
